```python
import math
import jax, jax.numpy as jnp
from jax import lax
import numpy as np

D_MODEL = 2048
BATCH = 1
SEQ = 16384
DEPTH = 1

CHUNK = 64
MIX_WIDTH = D_MODEL
RET_WIDTH = MIX_WIDTH // 2
LRU_WIDTH = MIX_WIDTH - RET_WIDTH
RET_HEADS = 8
RET_HEAD_DIM = RET_WIDTH // RET_HEADS
LRU_BLOCKS = 8
LRU_BLOCK_DIM = LRU_WIDTH // LRU_BLOCKS
CONV_WIDTH = 4
LRU_C = 8.0
ROPE_BASE = 10000.0
PEER_HEADS = 8
PEER_QUERY_DIM = 256
PEER_HALF = PEER_QUERY_DIM // 2
N_KEYS = 128
N_EXPERTS = N_KEYS * N_KEYS
PEER_TOPK = 16
PEER_TOKEN_BLOCK = 128
IN_COLS = 4 * RET_WIDTH + 2 * LRU_WIDTH
DN_ALPHA = (2.0 * DEPTH) ** 0.25
DN_BETA = (8.0 * DEPTH) ** -0.25
LN_EPS = 1e-5
GN_EPS = 1e-6

kernel_name = "hybrid_retention_rglru_peer_deepnorm"


def layer_norm(x, g, b):
    x32 = x.astype(jnp.float32)
    mu = jnp.mean(x32, axis=-1, keepdims=True)
    var = jnp.mean(jnp.square(x32 - mu), axis=-1, keepdims=True)
    return (x32 - mu) * lax.rsqrt(var + LN_EPS) * g + b


def rotary(x, positions):
    half = x.shape[-1] // 2
    inv_freq = ROPE_BASE ** (-jnp.arange(half, dtype=jnp.float32) / half)
    ang = positions.astype(jnp.float32)[..., None] * inv_freq
    cos = jnp.cos(ang)[:, :, None, :]
    sin = jnp.sin(ang)[:, :, None, :]
    x1, x2 = x[..., :half], x[..., half:]
    return jnp.concatenate([x1 * cos - x2 * sin, x2 * cos + x1 * sin], axis=-1)


def retention(q, k, v, g, positions):
    B, S = q.shape[:2]
    NC = S // CHUNK
    H, dh = RET_HEADS, RET_HEAD_DIM
    q = rotary(q.reshape(B, S, H, dh).astype(jnp.float32), positions) * (dh ** -0.5)
    k = rotary(k.reshape(B, S, H, dh).astype(jnp.float32), positions)
    v = v.reshape(B, S, H, dh).astype(jnp.float32)
    gamma = 1.0 - 2.0 ** (-5.0 - jnp.arange(H, dtype=jnp.float32))
    log_g = jnp.log(gamma)
    l = jnp.arange(CHUNK, dtype=jnp.float32)
    d_intra = jnp.exp(log_g[:, None, None] * jnp.abs(l[:, None] - l[None, :]))
    q_dec = jnp.exp(log_g[None, :] * (l[:, None] + 1.0))
    k_dec = jnp.exp(log_g[None, :] * (CHUNK - 1.0 - l[:, None]))
    chunk_dec = jnp.exp(log_g * CHUNK)
    qc = q.reshape(B, NC, CHUNK, H, dh)
    kc = k.reshape(B, NC, CHUNK, H, dh)
    vc = v.reshape(B, NC, CHUNK, H, dh)
    scores = jnp.einsum('bncht,bnmht->bnhcm', qc, kc) * d_intra
    o_intra = jnp.einsum('bnhcm,bnmhe->bnche', scores, vc)
    kv = jnp.einsum('bnmht,bnmhe->bnhte', kc * k_dec[:, :, None], vc)

    def step(state, kv_n):
        return chunk_dec[:, None, None] * state + kv_n, state

    _, states = lax.scan(step, jnp.zeros((B, H, dh, dh), jnp.float32), jnp.moveaxis(kv, 1, 0))
    states = jnp.moveaxis(states, 0, 1)
    o_cross = jnp.einsum('bncht,bnhte->bnche', qc * q_dec[:, :, None], states)
    o = (o_intra + o_cross).reshape(B, S, H, dh)
    mu = jnp.mean(o, axis=-1, keepdims=True)
    var = jnp.mean(jnp.square(o - mu), axis=-1, keepdims=True)
    o = ((o - mu) * lax.rsqrt(var + GN_EPS)).reshape(B, S, RET_WIDTH)
    return jax.nn.silu(g.astype(jnp.float32)) * o


def rglru(xr, gr, conv_w, conv_b, w_ra, b_ra, w_ri, b_ri, lam):
    B, S = xr.shape[:2]
    x = xr.astype(jnp.float32)
    pad = jnp.pad(x, ((0, 0), (CONV_WIDTH - 1, 0), (0, 0)))
    xc = conv_b + sum(conv_w[kk] * pad[:, kk:kk + S] for kk in range(CONV_WIDTH))
    xh = xc.reshape(B, S, LRU_BLOCKS, LRU_BLOCK_DIM)
    r = jax.nn.sigmoid(jnp.einsum('bsgi,gij->bsgj', xh, w_ra).reshape(B, S, LRU_WIDTH) + b_ra)
    i = jax.nn.sigmoid(jnp.einsum('bsgi,gij->bsgj', xh, w_ri).reshape(B, S, LRU_WIDTH) + b_ri)
    log_a = LRU_C * r * jax.nn.log_sigmoid(lam.astype(jnp.float32))
    a = jnp.exp(log_a)
    b = jnp.sqrt(-jnp.expm1(2.0 * log_a)) * (i * xc)

    def combine(e1, e2):
        a1, b1 = e1
        a2, b2 = e2
        return a1 * a2, a2 * b1 + b2

    _, h = lax.associative_scan(combine, (a, b), axis=1)
    return h * jax.nn.gelu(gr.astype(jnp.float32))


def peer(x, w_pq, sub_keys, peer_u, peer_v):
    B, S, D = x.shape
    xt = x.reshape(-1, PEER_TOKEN_BLOCK, D)
    K = PEER_TOPK

    def block(xb):
        q = (xb @ w_pq).reshape(PEER_TOKEN_BLOCK, PEER_HEADS, 2, PEER_HALF)
        s1 = jnp.einsum('thd,hnd->thn', q[:, :, 0], sub_keys[:, 0])
        s2 = jnp.einsum('thd,hnd->thn', q[:, :, 1], sub_keys[:, 1])
        v1, i1 = lax.top_k(s1, K)
        v2, i2 = lax.top_k(s2, K)
        cand = (v1[..., :, None] + v2[..., None, :]).reshape(PEER_TOKEN_BLOCK, PEER_HEADS, K * K)
        vs, ci = lax.top_k(cand, K)
        e1 = jnp.take_along_axis(i1, ci // K, axis=-1)
        e2 = jnp.take_along_axis(i2, ci % K, axis=-1)
        expert = e1 * N_KEYS + e2
        gates = jax.nn.softmax(vs.astype(jnp.float32), axis=-1)
        u = peer_u[expert]
        act = jax.nn.gelu(jnp.einsum('thkd,td->thk', u, xb))
        return jnp.einsum('thk,thkd->td', gates * act, peer_v[expert])

    return lax.map(block, xt).reshape(B, S, D)


def setup_inputs(seed: int = 0) -> dict:
    key = jax.random.key(seed)
    ks = jax.random.split(key, 18)
    f32 = jnp.float32
    x = jax.random.normal(ks[0], (BATCH, SEQ, D_MODEL), f32)
    positions = jnp.broadcast_to(jnp.arange(SEQ, dtype=jnp.int32), (BATCH, SEQ))
    col_scale = jnp.concatenate([
        jnp.ones((2 * RET_WIDTH,), f32), jnp.full((RET_WIDTH,), DN_BETA, f32),
        jnp.ones((RET_WIDTH,), f32), jnp.full((LRU_WIDTH,), DN_BETA, f32),
        jnp.ones((LRU_WIDTH,), f32)])
    w_in = jax.random.normal(ks[1], (DEPTH, D_MODEL, IN_COLS), f32) * (D_MODEL ** -0.5) * col_scale
    w_out = jax.random.normal(ks[2], (DEPTH, MIX_WIDTH, D_MODEL), f32) * (MIX_WIDTH ** -0.5) * DN_BETA
    conv_w = jax.random.normal(ks[3], (DEPTH, CONV_WIDTH, LRU_WIDTH), f32) * (CONV_WIDTH ** -0.5)
    conv_b = 0.01 * jax.random.normal(ks[4], (DEPTH, LRU_WIDTH), f32)
    w_ra = jax.random.normal(ks[5], (DEPTH, LRU_BLOCKS, LRU_BLOCK_DIM, LRU_BLOCK_DIM), f32) * (LRU_BLOCK_DIM ** -0.5)
    b_ra = 0.01 * jax.random.normal(ks[6], (DEPTH, LRU_WIDTH), f32)
    w_ri = jax.random.normal(ks[7], (DEPTH, LRU_BLOCKS, LRU_BLOCK_DIM, LRU_BLOCK_DIM), f32) * (LRU_BLOCK_DIM ** -0.5)
    b_ri = 0.01 * jax.random.normal(ks[8], (DEPTH, LRU_WIDTH), f32)
    a_c = jax.random.uniform(ks[9], (DEPTH, LRU_WIDTH), f32, minval=0.9, maxval=0.999)
    a0 = a_c ** (1.0 / LRU_C)
    lru_lambda = jnp.log(a0) - jnp.log1p(-a0)
    ln1_g = 1.0 + 0.02 * jax.random.normal(ks[10], (DEPTH, D_MODEL), f32)
    ln1_b = 0.02 * jax.random.normal(ks[11], (DEPTH, D_MODEL), f32)
    w_pq = jax.random.normal(ks[12], (DEPTH, D_MODEL, PEER_HEADS * PEER_QUERY_DIM), f32) * (D_MODEL ** -0.5)
    sub_keys = jax.random.normal(ks[13], (DEPTH, PEER_HEADS, 2, N_KEYS, PEER_HALF), f32) * (PEER_HALF ** -0.5)
    peer_u = jax.random.normal(ks[14], (DEPTH, N_EXPERTS, D_MODEL), f32) * (D_MODEL ** -0.5)
    peer_v = jax.random.normal(ks[15], (DEPTH, N_EXPERTS, D_MODEL), f32) * DN_BETA
    ln2_g = 1.0 + 0.02 * jax.random.normal(ks[16], (DEPTH, D_MODEL), f32)
    ln2_b = 0.02 * jax.random.normal(ks[17], (DEPTH, D_MODEL), f32)
    return {"x": x, "positions": positions, "w_in": w_in, "w_out": w_out,
            "conv_w": conv_w, "conv_b": conv_b, "w_ra": w_ra, "b_ra": b_ra,
            "w_ri": w_ri, "b_ri": b_ri, "lru_lambda": lru_lambda,
            "ln1_g": ln1_g, "ln1_b": ln1_b, "w_pq": w_pq, "sub_keys": sub_keys,
            "peer_u": peer_u, "peer_v": peer_v, "ln2_g": ln2_g, "ln2_b": ln2_b}


def reference(x, positions, w_in, w_out, conv_w, conv_b, w_ra, b_ra, w_ri, b_ri, lru_lambda,
              ln1_g, ln1_b, w_pq, sub_keys, peer_u, peer_v, ln2_g, ln2_b):
    R, L = RET_WIDTH, LRU_WIDTH
    h = x
    for l in range(DEPTH):
        proj = h @ w_in[l]
        q, k, v, g, xr, gr = jnp.split(proj, [R, 2 * R, 3 * R, 4 * R, 4 * R + L], axis=-1)
        y_ret = retention(q, k, v, g, positions)
        y_lru = rglru(xr, gr, conv_w[l], conv_b[l], w_ra[l], b_ra[l], w_ri[l], b_ri[l], lru_lambda[l])
        mix = jnp.concatenate([y_ret, y_lru], axis=-1) @ w_out[l]
        h = layer_norm(DN_ALPHA * h + mix, ln1_g[l], ln1_b[l])
        h = layer_norm(DN_ALPHA * h + peer(h, w_pq[l], sub_keys[l], peer_u[l], peer_v[l]), ln2_g[l], ln2_b[l])
    return h
```

```python
import functools

import jax
import jax.numpy as jnp
from jax import lax
from jax.experimental import pallas as pl
from jax.experimental.pallas import tpu as pltpu

F32 = jnp.float32
BF16 = jnp.bfloat16

CHUNK = 64
RET_HEADS = 8
HEAD_DIM = 128
LRU_BLOCKS = 8
LRU_BLOCK_DIM = 128
CONV_WIDTH = 4
LRU_C = 8.0
ROPE_BASE = 10000.0
PEER_HEADS = 8
PEER_HALF = 128
N_KEYS = 128
PEER_TOPK = 16
LN_EPS = 1e-5
GN_EPS = 1e-6

LANES = 128
SUBLANES = 8
VMEM_LIMIT = 56 * 1024 * 1024

SEQ_BLOCK = 256
PEER_BLOCK = 128
GATHER_SLOTS = 4


def _cparams(n_axes):
    return pltpu.CompilerParams(dimension_semantics=("arbitrary",) * n_axes,
                                vmem_limit_bytes=VMEM_LIMIT)


def _proj_kernel(x_ref, w_ref, o_ref, xb_ref):
    @pl.when(pl.program_id(1) == 0)
    def _():
        xb_ref[...] = x_ref[...].astype(BF16)

    o_ref[...] = jnp.dot(xb_ref[...], w_ref[...], preferred_element_type=F32)


def _proj(x, w_bf16):
    s, d = x.shape
    n = w_bf16.shape[1]
    tm = min(1024, s)
    tn = 1024
    return pl.pallas_call(
        _proj_kernel,
        grid=(s // tm, n // tn),
        in_specs=[pl.BlockSpec((tm, d), lambda i, j: (i, 0)),
                  pl.BlockSpec((d, tn), lambda i, j: (0, j))],
        out_specs=pl.BlockSpec((tm, tn), lambda i, j: (i, j)),
        out_shape=jax.ShapeDtypeStruct((s, n), F32),
        scratch_shapes=[pltpu.VMEM((tm, d), BF16)],
        compiler_params=_cparams(2),
        name="proj",
    )(x, w_bf16)


def _retention_kernel(pos_ref, q_ref, k_ref, v_ref, g_ref, invf_ref, dmat_ref, qdec_ref,
                      kdec_ref, sdec_ref, o_ref, state_ref):
    @pl.when(pl.program_id(0) == 0)
    def _():
        state_ref[...] = jnp.zeros_like(state_ref)

    t = q_ref.shape[0]
    ang = pos_ref[...].astype(F32) * invf_ref[...]
    cosv = jnp.cos(ang)
    lane = lax.broadcasted_iota(jnp.int32, (t, LANES), 1)
    sinm = jnp.where(lane < HEAD_DIM // 2, -1.0, 1.0) * jnp.sin(ang)
    scale = HEAD_DIM ** -0.5
    for h in range(RET_HEADS):
        sl = slice(h * HEAD_DIM, (h + 1) * HEAD_DIM)
        qh = q_ref[:, sl]
        kh = k_ref[:, sl]
        qr = (qh * cosv + pltpu.roll(qh, HEAD_DIM // 2, 1) * sinm) * scale
        kr = kh * cosv + pltpu.roll(kh, HEAD_DIM // 2, 1) * sinm
        vb = v_ref[:, sl].astype(BF16)
        scores = lax.dot_general(qr.astype(BF16), kr.astype(BF16), (((1,), (1,)), ((), ())),
                                 preferred_element_type=F32)
        scores = scores * dmat_ref[h]
        st = state_ref[h]
        o = jnp.dot(scores.astype(BF16), vb, preferred_element_type=F32)
        o = o + jnp.dot((qr * qdec_ref[h]).astype(BF16), st.astype(BF16),
                        preferred_element_type=F32)
        kd = (kr * kdec_ref[h]).astype(BF16)
        kv = lax.dot_general(kd, vb, (((0,), (0,)), ((), ())), preferred_element_type=F32)
        state_ref[h] = st * sdec_ref[h] + kv
        mu = jnp.mean(o, axis=-1, keepdims=True)
        oc = o - mu
        var = jnp.mean(oc * oc, axis=-1, keepdims=True)
        gh = g_ref[:, sl]
        o_ref[:, sl] = (gh * jax.nn.sigmoid(gh)) * (oc * lax.rsqrt(var + GN_EPS))


def _retention_consts(t):
    h = RET_HEADS
    gamma = 1.0 - 2.0 ** (-5.0 - jnp.arange(h, dtype=F32))
    log_g = jnp.log(gamma)
    l = jnp.arange(t, dtype=F32)
    dist = jnp.abs(l[:, None] - l[None, :])
    chunk = jnp.arange(t, dtype=jnp.int32) // CHUNK
    visible = chunk[:, None] >= chunk[None, :]
    dmat = jnp.where(visible[None], jnp.exp(log_g[:, None, None] * dist[None]), 0.0)
    qdec = jnp.exp(log_g[:, None] * (l[None, :] + 1.0))
    kdec = jnp.exp(log_g[:, None] * (t - 1.0 - l[None, :]))
    sdec = jnp.exp(log_g * t)
    bc = lambda a: jnp.broadcast_to(a[:, :, None], (h, a.shape[1], LANES)).astype(F32)
    return dmat.astype(F32), bc(qdec), bc(kdec), bc(sdec[:, None])


def _retention(proj, positions_col):
    s = proj.shape[0]
    t = min(SEQ_BLOCK, s)
    rw = RET_HEADS * HEAD_DIM
    half = HEAD_DIM // 2
    inv_freq = ROPE_BASE ** (-jnp.arange(half, dtype=F32) / half)
    invf = jnp.concatenate([inv_freq, inv_freq])[None, :]
    dmat, qdec, kdec, sdec = _retention_consts(t)
    const = lambda shape: pl.BlockSpec(shape, lambda i: (0,) * len(shape))
    col = lambda c: pl.BlockSpec((t, rw), lambda i, c=c: (i, c))
    return pl.pallas_call(
        _retention_kernel,
        grid=(s // t,),
        in_specs=[pl.BlockSpec((t, 1), lambda i: (i, 0)), col(0), col(1), col(2), col(3),
                  const((1, LANES)), const((RET_HEADS, t, t)), const((RET_HEADS, t, LANES)),
                  const((RET_HEADS, t, LANES)), const((RET_HEADS, 1, LANES))],
        out_specs=pl.BlockSpec((t, rw), lambda i: (i, 0)),
        out_shape=jax.ShapeDtypeStruct((s, rw), F32),
        scratch_shapes=[pltpu.VMEM((RET_HEADS, HEAD_DIM, HEAD_DIM), F32)],
        compiler_params=_cparams(1),
        name="retention",
    )(positions_col, proj, proj, proj, proj, invf, dmat, qdec, kdec, sdec)


def _shift_rows_in_group(x, d, fill, sub):
    return jnp.where(sub >= d, pltpu.roll(x, d, 1), fill)


def _rglru_kernel(xr_ref, gr_ref, cw_ref, cb_ref, wcat_ref, bra_ref, bri_ref, lam_ref, o_ref,
                  xext_ref, hprev_ref):
    t = xr_ref.shape[0]
    groups = t // SUBLANES

    @pl.when(pl.program_id(0) == 0)
    def _():
        xext_ref[0:SUBLANES, :] = jnp.zeros((SUBLANES, xext_ref.shape[1]), F32)
        hprev_ref[...] = jnp.zeros_like(hprev_ref)

    x = xr_ref[...]
    xext_ref[SUBLANES:, :] = x
    xc = cb_ref[...] + (cw_ref[0:1, :] * xext_ref[pl.ds(SUBLANES - 3, t), :]
                        + cw_ref[1:2, :] * xext_ref[pl.ds(SUBLANES - 2, t), :]
                        + cw_ref[2:3, :] * xext_ref[pl.ds(SUBLANES - 1, t), :]
                        + cw_ref[3:4, :] * x)
    xext_ref[0:SUBLANES, :] = x[t - SUBLANES:, :]

    lam = lam_ref[...]
    log_sig = jnp.minimum(lam, 0.0) - jnp.log1p(jnp.exp(-jnp.abs(lam)))
    sub = lax.broadcasted_iota(jnp.int32, (groups, SUBLANES, LANES), 1)
    for g in range(LRU_BLOCKS):
        sl = slice(g * LRU_BLOCK_DIM, (g + 1) * LRU_BLOCK_DIM)
        xg = xc[:, sl]
        ri = jnp.dot(xg.astype(BF16), wcat_ref[g], preferred_element_type=F32)
        r = jax.nn.sigmoid(ri[:, :LRU_BLOCK_DIM] + bra_ref[:, sl])
        gate_i = jax.nn.sigmoid(ri[:, LRU_BLOCK_DIM:] + bri_ref[:, sl])
        log_a = LRU_C * r * log_sig[:, sl]
        a = jnp.exp(log_a)
        b = jnp.sqrt(-jnp.tanh(log_a) * (a * a + 1.0)) * (gate_i * xg)
        a3 = a.reshape(groups, SUBLANES, LANES)
        b3 = b.reshape(groups, SUBLANES, LANES)
        for d in (1, 2, 4):
            a_sh = _shift_rows_in_group(a3, d, 1.0, sub)
            b_sh = _shift_rows_in_group(b3, d, 0.0, sub)
            b3 = a3 * b_sh + b3
            a3 = a3 * a_sh
        hcar = hprev_ref[:, sl]
        rows = []
        for gi in range(groups):
            hg = a3[gi] * hcar + b3[gi]
            rows.append(hg)
            hcar = hg[SUBLANES - 1:SUBLANES, :]
        hprev_ref[:, sl] = hcar
        hfull = jnp.concatenate(rows, axis=0)
        o_ref[:, sl] = hfull * jax.nn.gelu(gr_ref[:, sl])


def _rglru(proj, conv_w, conv_b, wcat, b_ra, b_ri, lam):
    s = proj.shape[0]
    t = min(SEQ_BLOCK, s)
    lw = LRU_BLOCKS * LRU_BLOCK_DIM
    const = lambda shape: pl.BlockSpec(shape, lambda i: (0,) * len(shape))
    col = lambda c: pl.BlockSpec((t, lw), lambda i, c=c: (i, c))
    return pl.pallas_call(
        _rglru_kernel,
        grid=(s // t,),
        in_specs=[col(4), col(5), const((CONV_WIDTH, lw)), const((1, lw)),
                  const((LRU_BLOCKS, LRU_BLOCK_DIM, 2 * LRU_BLOCK_DIM)),
                  const((1, lw)), const((1, lw)), const((1, lw))],
        out_specs=pl.BlockSpec((t, lw), lambda i: (i, 0)),
        out_shape=jax.ShapeDtypeStruct((s, lw), F32),
        scratch_shapes=[pltpu.VMEM((t + SUBLANES, lw), F32), pltpu.VMEM((1, lw), F32)],
        compiler_params=_cparams(1),
        name="rglru",
    )(proj, proj, conv_w, conv_b[None, :], wcat, b_ra[None, :], b_ri[None, :], lam[None, :])


def _layer_norm(x, g, b):
    mu = jnp.mean(x, axis=-1, keepdims=True)
    xc = x - mu
    var = jnp.mean(xc * xc, axis=-1, keepdims=True)
    return xc * lax.rsqrt(var + LN_EPS) * g + b


def _mix_kernel(alpha, x_ref, yr_ref, yl_ref, wo_ref, g_ref, b_ref, wpq_ref, h_ref, q_ref):
    rw = yr_ref.shape[1]
    mix = (jnp.dot(yr_ref[...].astype(BF16), wo_ref[0:rw, :], preferred_element_type=F32)
           + jnp.dot(yl_ref[...].astype(BF16), wo_ref[rw:, :], preferred_element_type=F32))
    h1 = _layer_norm(alpha * x_ref[...] + mix, g_ref[...], b_ref[...])
    h_ref[...] = h1
    q_ref[...] = jnp.dot(h1.astype(BF16), wpq_ref[...], preferred_element_type=F32)


def _mix(alpha, x, y_ret, y_lru, w_out_bf16, ln_g, ln_b, w_pq_bf16):
    s, d = x.shape
    t = min(SEQ_BLOCK, s)
    rw, lw, nq = y_ret.shape[1], y_lru.shape[1], w_pq_bf16.shape[1]
    const = lambda shape: pl.BlockSpec(shape, lambda i: (0,) * len(shape),
                                       pipeline_mode=pl.Buffered(1))
    row = lambda w: pl.BlockSpec((t, w), lambda i: (i, 0))
    return pl.pallas_call(
        functools.partial(_mix_kernel, alpha),
        grid=(s // t,),
        in_specs=[row(d), row(rw), row(lw), const((rw + lw, d)), const((1, d)), const((1, d)),
                  const((d, nq))],
        out_specs=[row(d), row(nq)],
        out_shape=[jax.ShapeDtypeStruct((s, d), F32), jax.ShapeDtypeStruct((s, nq), F32)],
        compiler_params=_cparams(1),
        name="mix",
    )(x, y_ret, y_lru, w_out_bf16, ln_g[None, :], ln_b[None, :], w_pq_bf16)


NEG_INF = float("-inf")


def _staircase():
    k = PEER_TOPK
    return [(i, j) for i in range(k) for j in range(k) if (i + 1) * (j + 1) <= k]


def _topk_kernel(q_ref, sk_ref, e1_ref, e2_ref, cmask_ref, ids_ref, gates_ref):
    h = pl.program_id(1)
    t = q_ref.shape[0]
    k = PEER_TOPK
    lane = lax.broadcasted_iota(jnp.int32, (t, LANES), 1)
    lane_f = lane.astype(F32)

    def top16(sc, payload):
        vals = jnp.zeros((t, LANES), F32)
        idxs = jnp.zeros((t, LANES), F32)
        for it in range(k):
            m = jnp.max(sc, axis=-1, keepdims=True)
            pos = jnp.min(jnp.where(sc == m, lane_f, float(LANES)), axis=-1, keepdims=True)
            hit = lane_f == pos
            if payload is None:
                out = pos
            else:
                out = jnp.sum(jnp.where(hit, payload, 0.0), axis=-1, keepdims=True)
            vals = jnp.where(lane == it, m, vals)
            idxs = jnp.where(lane == it, out, idxs)
            sc = jnp.where(hit, NEG_INF, sc)
        return vals, idxs

    nt = (((1,), (1,)), ((), ()))
    s1 = lax.dot_general(q_ref[:, :PEER_HALF].astype(BF16), sk_ref[0, 0], nt,
                         preferred_element_type=F32)
    s2 = lax.dot_general(q_ref[:, PEER_HALF:].astype(BF16), sk_ref[0, 1], nt,
                         preferred_element_type=F32)
    v1, i1 = top16(s1, None)
    v2, i2 = top16(s2, None)
    hp = lax.Precision.HIGHEST
    cand = (jnp.dot(v1, e1_ref[...], precision=hp, preferred_element_type=F32)
            + jnp.dot(v2, e2_ref[...], precision=hp, preferred_element_type=F32)
            + cmask_ref[...])
    expert = (jnp.dot(i1, e1_ref[...], precision=hp, preferred_element_type=F32) * float(N_KEYS)
              + jnp.dot(i2, e2_ref[...], precision=hp, preferred_element_type=F32))
    vs, es = top16(cand, expert)
    valid = lane < k
    p = jnp.where(valid, jnp.exp(vs - vs[:, 0:1]), 0.0)
    gate = p / jnp.sum(p, axis=-1, keepdims=True)

    @pl.when(h == 0)
    def _():
        ids_ref[...] = jnp.zeros_like(ids_ref)
        gates_ref[...] = jnp.zeros_like(gates_ref)

    shift = h * k
    own = (lane >= shift) & (lane < shift + k)
    gates_ref[...] = jnp.where(own, pltpu.roll(gate, shift, 1), gates_ref[...])
    ids_ref[...] = jnp.where(own, pltpu.roll(es, shift, 1).astype(jnp.int32), ids_ref[...])


def _topk(q, sub_keys_bf16):
    s = q.shape[0]
    t = min(SEQ_BLOCK, s)
    k = PEER_TOPK
    pairs = _staircase()
    lanes = jnp.arange(LANES)
    pi = jnp.array([p[0] for p in pairs] + [0] * (LANES - len(pairs)))
    pj = jnp.array([p[1] for p in pairs] + [0] * (LANES - len(pairs)))
    live = lanes < len(pairs)
    rows = jnp.arange(LANES)[:, None]
    e1 = ((rows == pi[None, :]) & live[None, :]).astype(F32)
    e2 = ((rows == pj[None, :]) & live[None, :]).astype(F32)
    cmask = jnp.where(live, 0.0, NEG_INF).astype(F32)[None, :]
    const = lambda shape: pl.BlockSpec(shape, lambda i, h: (0,) * len(shape))
    out = pl.BlockSpec((t, LANES), lambda i, h: (i, 0))
    return pl.pallas_call(
        _topk_kernel,
        grid=(s // t, PEER_HEADS),
        in_specs=[pl.BlockSpec((t, 2 * PEER_HALF), lambda i, h: (i, h)),
                  pl.BlockSpec((1, 2, N_KEYS, PEER_HALF), lambda i, h: (h, 0, 0, 0)),
                  const((LANES, LANES)), const((LANES, LANES)), const((1, LANES))],
        out_specs=[out, out],
        out_shape=[jax.ShapeDtypeStruct((s, PEER_HEADS * k), jnp.int32),
                   jax.ShapeDtypeStruct((s, PEER_HEADS * k), F32)],
        compiler_params=_cparams(2),
        name="topk",
    )(q, sub_keys_bf16, e1, e2, cmask)


def _peer_kernel(alpha, ids_ref, gates_ref, h_ref, u_hbm, v_hbm, g_ref, b_ref, o_ref,
                 ubuf, vbuf, acc_ref, sem):
    tb = h_ref.shape[0]
    n_rows = ubuf.shape[1]

    def row_copies(tok, slot, r):
        e = ids_ref[tok, r]
        return (pltpu.make_async_copy(u_hbm.at[pl.ds(e, 1), :], ubuf.at[slot, pl.ds(r, 1), :],
                                      sem.at[0, slot]),
                pltpu.make_async_copy(v_hbm.at[pl.ds(e, 1), :], vbuf.at[slot, pl.ds(r, 1), :],
                                      sem.at[1, slot]))

    def issue(tok, slot):
        def body(r, carry):
            cu, cv = row_copies(tok, slot, r)
            cu.start()
            cv.start()
            return carry
        lax.fori_loop(0, n_rows, body, 0, unroll=8)

    def wait(slot):
        pltpu.make_async_copy(u_hbm.at[pl.ds(0, n_rows), :], ubuf.at[slot], sem.at[0, slot]).wait()
        pltpu.make_async_copy(v_hbm.at[pl.ds(0, n_rows), :], vbuf.at[slot], sem.at[1, slot]).wait()

    for tok in range(GATHER_SLOTS - 1):
        issue(tok, tok)

    def token(tok, carry):
        slot = lax.rem(tok, GATHER_SLOTS)
        nxt = tok + (GATHER_SLOTS - 1)

        @pl.when(nxt < tb)
        def _():
            issue(nxt, lax.rem(nxt, GATHER_SLOTS))

        wait(slot)
        xb = h_ref[pl.ds(tok, 1), :].astype(BF16)
        act = lax.dot_general(xb, ubuf[slot].astype(BF16), (((1,), (1,)), ((), ())),
                              preferred_element_type=F32)
        w = gates_ref[pl.ds(tok, 1), :] * jax.nn.gelu(act)
        acc_ref[pl.ds(tok, 1), :] = jnp.dot(w.astype(BF16), vbuf[slot].astype(BF16),
                                            preferred_element_type=F32)
        return carry

    lax.fori_loop(0, tb, token, 0)
    o_ref[...] = _layer_norm(alpha * h_ref[...] + acc_ref[...], g_ref[...], b_ref[...])


def _peer(alpha, ids, gates, h1, peer_u, peer_v, ln_g, ln_b):
    s, d = h1.shape
    tb = min(PEER_BLOCK, s)
    n_rows = ids.shape[1]
    row = lambda w: pl.BlockSpec((tb, w), lambda i: (i, 0))
    const = lambda shape: pl.BlockSpec(shape, lambda i: (0,) * len(shape))
    return pl.pallas_call(
        functools.partial(_peer_kernel, alpha),
        grid=(s // tb,),
        in_specs=[pl.BlockSpec((tb, n_rows), lambda i: (i, 0), memory_space=pltpu.SMEM),
                  row(n_rows), row(d),
                  pl.BlockSpec(memory_space=pl.ANY), pl.BlockSpec(memory_space=pl.ANY),
                  const((1, d)), const((1, d))],
        out_specs=row(d),
        out_shape=jax.ShapeDtypeStruct((s, d), F32),
        scratch_shapes=[pltpu.VMEM((GATHER_SLOTS, n_rows, d), F32),
                        pltpu.VMEM((GATHER_SLOTS, n_rows, d), F32),
                        pltpu.VMEM((tb, d), F32),
                        pltpu.SemaphoreType.DMA((2, GATHER_SLOTS))],
        compiler_params=_cparams(1),
        name="peer",
    )(ids, gates, h1, peer_u, peer_v, ln_g[None, :], ln_b[None, :])


def kernel(x, positions, w_in, w_out, conv_w, conv_b, w_ra, b_ra, w_ri, b_ri, lru_lambda,
           ln1_g, ln1_b, w_pq, sub_keys, peer_u, peer_v, ln2_g, ln2_b):
    batch, seq, d = x.shape
    assert batch == 1, "one sequence per call"
    depth = w_in.shape[0]
    alpha = (2.0 * depth) ** 0.25
    pos_col = positions.reshape(seq, 1)
    h = x.reshape(seq, d)
    for l in range(depth):
        proj = _proj(h, w_in[l].astype(BF16))
        y_ret = _retention(proj, pos_col)
        wcat = jnp.concatenate([w_ra[l], w_ri[l]], axis=-1).astype(BF16)
        y_lru = _rglru(proj, conv_w[l], conv_b[l], wcat, b_ra[l], b_ri[l], lru_lambda[l])
        h1, q = _mix(alpha, h, y_ret, y_lru, w_out[l].astype(BF16), ln1_g[l], ln1_b[l],
                     w_pq[l].astype(BF16))
        ids, gates = _topk(q, sub_keys[l].astype(BF16))
        h = _peer(alpha, ids, gates, h1, peer_u[l], peer_v[l], ln2_g[l], ln2_b[l])
    return h.reshape(batch, seq, d)
```

```python
import functools

import jax
import jax.numpy as jnp
from jax import lax
from jax.experimental import pallas as pl
from jax.experimental.pallas import tpu as pltpu

F32 = jnp.float32
BF16 = jnp.bfloat16

CHUNK = 64
RET_HEADS = 8
HEAD_DIM = 128
LRU_BLOCKS = 8
LRU_BLOCK_DIM = 128
CONV_WIDTH = 4
LRU_C = 8.0
ROPE_BASE = 10000.0
PEER_HEADS = 8
PEER_HALF = 128
N_KEYS = 128
PEER_TOPK = 16
LN_EPS = 1e-5
GN_EPS = 1e-6

LANES = 128
SUBLANES = 8
VMEM_LIMIT = 56 * 1024 * 1024

SEQ_BLOCK = 256
PEER_BLOCK = 128
GATHER_GROUP = 4


def _cparams(n_axes):
    return pltpu.CompilerParams(dimension_semantics=("arbitrary",) * n_axes,
                                vmem_limit_bytes=VMEM_LIMIT)


def _proj_kernel(x_ref, w_ref, o_ref, xb_ref):
    @pl.when(pl.program_id(1) == 0)
    def _():
        xb_ref[...] = x_ref[...].astype(BF16)

    o_ref[...] = jnp.dot(xb_ref[...], w_ref[...], preferred_element_type=F32)


def _proj(x, w_bf16):
    s, d = x.shape
    n = w_bf16.shape[1]
    tm = min(1024, s)
    tn = 1024
    return pl.pallas_call(
        _proj_kernel,
        grid=(s // tm, n // tn),
        in_specs=[pl.BlockSpec((tm, d), lambda i, j: (i, 0)),
                  pl.BlockSpec((d, tn), lambda i, j: (0, j))],
        out_specs=pl.BlockSpec((tm, tn), lambda i, j: (i, j)),
        out_shape=jax.ShapeDtypeStruct((s, n), F32),
        scratch_shapes=[pltpu.VMEM((tm, d), BF16)],
        compiler_params=_cparams(2),
        name="proj",
    )(x, w_bf16)


def _retention_kernel(pos_ref, q_ref, k_ref, v_ref, g_ref, invf_ref, dmat_ref, qdec_ref,
                      kdec_ref, sdec_ref, o_ref, state_ref):
    @pl.when(pl.program_id(0) == 0)
    def _():
        state_ref[...] = jnp.zeros_like(state_ref)

    t = q_ref.shape[0]
    ang = pos_ref[...].astype(F32) * invf_ref[...]
    cosv = jnp.cos(ang)
    lane = lax.broadcasted_iota(jnp.int32, (t, LANES), 1)
    sinm = jnp.where(lane < HEAD_DIM // 2, -1.0, 1.0) * jnp.sin(ang)
    scale = HEAD_DIM ** -0.5
    for h in range(RET_HEADS):
        sl = slice(h * HEAD_DIM, (h + 1) * HEAD_DIM)
        qh = q_ref[:, sl]
        kh = k_ref[:, sl]
        qr = (qh * cosv + pltpu.roll(qh, HEAD_DIM // 2, 1) * sinm) * scale
        kr = kh * cosv + pltpu.roll(kh, HEAD_DIM // 2, 1) * sinm
        vb = v_ref[:, sl].astype(BF16)
        scores = lax.dot_general(qr.astype(BF16), kr.astype(BF16), (((1,), (1,)), ((), ())),
                                 preferred_element_type=F32)
        scores = scores * dmat_ref[h]
        st = state_ref[h]
        o = jnp.dot(scores.astype(BF16), vb, preferred_element_type=F32)
        o = o + jnp.dot((qr * qdec_ref[h]).astype(BF16), st.astype(BF16),
                        preferred_element_type=F32)
        kd = (kr * kdec_ref[h]).astype(BF16)
        kv = lax.dot_general(kd, vb, (((0,), (0,)), ((), ())), preferred_element_type=F32)
        state_ref[h] = st * sdec_ref[h] + kv
        mu = jnp.mean(o, axis=-1, keepdims=True)
        oc = o - mu
        var = jnp.mean(oc * oc, axis=-1, keepdims=True)
        gh = g_ref[:, sl]
        o_ref[:, sl] = (gh * jax.nn.sigmoid(gh)) * (oc * lax.rsqrt(var + GN_EPS))


def _retention_consts(t):
    h = RET_HEADS
    gamma = 1.0 - 2.0 ** (-5.0 - jnp.arange(h, dtype=F32))
    log_g = jnp.log(gamma)
    l = jnp.arange(t, dtype=F32)
    dist = jnp.abs(l[:, None] - l[None, :])
    chunk = jnp.arange(t, dtype=jnp.int32) // CHUNK
    visible = chunk[:, None] >= chunk[None, :]
    dmat = jnp.where(visible[None], jnp.exp(log_g[:, None, None] * dist[None]), 0.0)
    qdec = jnp.exp(log_g[:, None] * (l[None, :] + 1.0))
    kdec = jnp.exp(log_g[:, None] * (t - 1.0 - l[None, :]))
    sdec = jnp.exp(log_g * t)
    bc = lambda a: jnp.broadcast_to(a[:, :, None], (h, a.shape[1], LANES)).astype(F32)
    return dmat.astype(F32), bc(qdec), bc(kdec), bc(sdec[:, None])


def _retention(proj, positions_col):
    s = proj.shape[0]
    t = min(SEQ_BLOCK, s)
    rw = RET_HEADS * HEAD_DIM
    half = HEAD_DIM // 2
    inv_freq = ROPE_BASE ** (-jnp.arange(half, dtype=F32) / half)
    invf = jnp.concatenate([inv_freq, inv_freq])[None, :]
    dmat, qdec, kdec, sdec = _retention_consts(t)
    const = lambda shape: pl.BlockSpec(shape, lambda i: (0,) * len(shape))
    col = lambda c: pl.BlockSpec((t, rw), lambda i, c=c: (i, c))
    return pl.pallas_call(
        _retention_kernel,
        grid=(s // t,),
        in_specs=[pl.BlockSpec((t, 1), lambda i: (i, 0)), col(0), col(1), col(2), col(3),
                  const((1, LANES)), const((RET_HEADS, t, t)), const((RET_HEADS, t, LANES)),
                  const((RET_HEADS, t, LANES)), const((RET_HEADS, 1, LANES))],
        out_specs=pl.BlockSpec((t, rw), lambda i: (i, 0)),
        out_shape=jax.ShapeDtypeStruct((s, rw), F32),
        scratch_shapes=[pltpu.VMEM((RET_HEADS, HEAD_DIM, HEAD_DIM), F32)],
        compiler_params=_cparams(1),
        name="retention",
    )(positions_col, proj, proj, proj, proj, invf, dmat, qdec, kdec, sdec)


def _shift_rows_in_group(x, d, fill, sub):
    return jnp.where(sub >= d, pltpu.roll(x, d, 1), fill)


def _rglru_kernel(xr_ref, gr_ref, cw_ref, cb_ref, wcat_ref, bra_ref, bri_ref, lam_ref, o_ref,
                  xext_ref, hprev_ref):
    t = xr_ref.shape[0]
    groups = t // SUBLANES

    @pl.when(pl.program_id(0) == 0)
    def _():
        xext_ref[0:SUBLANES, :] = jnp.zeros((SUBLANES, xext_ref.shape[1]), F32)
        hprev_ref[...] = jnp.zeros_like(hprev_ref)

    x = xr_ref[...]
    xext_ref[SUBLANES:, :] = x
    xc = cb_ref[...] + (cw_ref[0:1, :] * xext_ref[pl.ds(SUBLANES - 3, t), :]
                        + cw_ref[1:2, :] * xext_ref[pl.ds(SUBLANES - 2, t), :]
                        + cw_ref[2:3, :] * xext_ref[pl.ds(SUBLANES - 1, t), :]
                        + cw_ref[3:4, :] * x)
    xext_ref[0:SUBLANES, :] = x[t - SUBLANES:, :]

    lam = lam_ref[...]
    log_sig = jnp.minimum(lam, 0.0) - jnp.log1p(jnp.exp(-jnp.abs(lam)))
    sub = lax.broadcasted_iota(jnp.int32, (groups, SUBLANES, LANES), 1)
    for g in range(LRU_BLOCKS):
        sl = slice(g * LRU_BLOCK_DIM, (g + 1) * LRU_BLOCK_DIM)
        xg = xc[:, sl]
        ri = jnp.dot(xg.astype(BF16), wcat_ref[g], preferred_element_type=F32)
        r = jax.nn.sigmoid(ri[:, :LRU_BLOCK_DIM] + bra_ref[:, sl])
        gate_i = jax.nn.sigmoid(ri[:, LRU_BLOCK_DIM:] + bri_ref[:, sl])
        log_a = LRU_C * r * log_sig[:, sl]
        a = jnp.exp(log_a)
        b = jnp.sqrt(-jnp.tanh(log_a) * (a * a + 1.0)) * (gate_i * xg)
        a3 = a.reshape(groups, SUBLANES, LANES)
        b3 = b.reshape(groups, SUBLANES, LANES)
        for d in (1, 2, 4):
            a_sh = _shift_rows_in_group(a3, d, 1.0, sub)
            b_sh = _shift_rows_in_group(b3, d, 0.0, sub)
            b3 = a3 * b_sh + b3
            a3 = a3 * a_sh
        hcar = hprev_ref[:, sl]
        rows = []
        for gi in range(groups):
            hg = a3[gi] * hcar + b3[gi]
            rows.append(hg)
            hcar = hg[SUBLANES - 1:SUBLANES, :]
        hprev_ref[:, sl] = hcar
        hfull = jnp.concatenate(rows, axis=0)
        o_ref[:, sl] = hfull * jax.nn.gelu(gr_ref[:, sl])


def _rglru(proj, conv_w, conv_b, wcat, b_ra, b_ri, lam):
    s = proj.shape[0]
    t = min(SEQ_BLOCK, s)
    lw = LRU_BLOCKS * LRU_BLOCK_DIM
    const = lambda shape: pl.BlockSpec(shape, lambda i: (0,) * len(shape))
    col = lambda c: pl.BlockSpec((t, lw), lambda i, c=c: (i, c))
    return pl.pallas_call(
        _rglru_kernel,
        grid=(s // t,),
        in_specs=[col(4), col(5), const((CONV_WIDTH, lw)), const((1, lw)),
                  const((LRU_BLOCKS, LRU_BLOCK_DIM, 2 * LRU_BLOCK_DIM)),
                  const((1, lw)), const((1, lw)), const((1, lw))],
        out_specs=pl.BlockSpec((t, lw), lambda i: (i, 0)),
        out_shape=jax.ShapeDtypeStruct((s, lw), F32),
        scratch_shapes=[pltpu.VMEM((t + SUBLANES, lw), F32), pltpu.VMEM((1, lw), F32)],
        compiler_params=_cparams(1),
        name="rglru",
    )(proj, proj, conv_w, conv_b[None, :], wcat, b_ra[None, :], b_ri[None, :], lam[None, :])


def _layer_norm(x, g, b):
    mu = jnp.mean(x, axis=-1, keepdims=True)
    xc = x - mu
    var = jnp.mean(xc * xc, axis=-1, keepdims=True)
    return xc * lax.rsqrt(var + LN_EPS) * g + b


def _mix_kernel(alpha, x_ref, yr_ref, yl_ref, wo_ref, g_ref, b_ref, wpq_ref, h_ref, q_ref):
    rw = yr_ref.shape[1]
    mix = (jnp.dot(yr_ref[...].astype(BF16), wo_ref[0:rw, :], preferred_element_type=F32)
           + jnp.dot(yl_ref[...].astype(BF16), wo_ref[rw:, :], preferred_element_type=F32))
    h1 = _layer_norm(alpha * x_ref[...] + mix, g_ref[...], b_ref[...])
    h_ref[...] = h1
    q_ref[...] = jnp.dot(h1.astype(BF16), wpq_ref[...], preferred_element_type=F32)


def _mix(alpha, x, y_ret, y_lru, w_out_bf16, ln_g, ln_b, w_pq_bf16):
    s, d = x.shape
    t = min(SEQ_BLOCK, s)
    rw, lw, nq = y_ret.shape[1], y_lru.shape[1], w_pq_bf16.shape[1]
    const = lambda shape: pl.BlockSpec(shape, lambda i: (0,) * len(shape),
                                       pipeline_mode=pl.Buffered(1))
    row = lambda w: pl.BlockSpec((t, w), lambda i: (i, 0))
    return pl.pallas_call(
        functools.partial(_mix_kernel, alpha),
        grid=(s // t,),
        in_specs=[row(d), row(rw), row(lw), const((rw + lw, d)), const((1, d)), const((1, d)),
                  const((d, nq))],
        out_specs=[row(d), row(nq)],
        out_shape=[jax.ShapeDtypeStruct((s, d), F32), jax.ShapeDtypeStruct((s, nq), F32)],
        compiler_params=_cparams(1),
        name="mix",
    )(x, y_ret, y_lru, w_out_bf16, ln_g[None, :], ln_b[None, :], w_pq_bf16)


NEG_INF = float("-inf")


def _staircase():
    k = PEER_TOPK
    return [(i, j) for i in range(k) for j in range(k) if (i + 1) * (j + 1) <= k]


def _topk_kernel(q_ref, sk_ref, e1_ref, e2_ref, cmask_ref, ids_ref, gates_ref):
    h = pl.program_id(1)
    t = q_ref.shape[0]
    k = PEER_TOPK
    lane = lax.broadcasted_iota(jnp.int32, (t, LANES), 1)
    lane_f = lane.astype(F32)

    def top16(sc, payload):
        vals = jnp.zeros((t, LANES), F32)
        idxs = jnp.zeros((t, LANES), F32)
        for it in range(k):
            m = jnp.max(sc, axis=-1, keepdims=True)
            pos = jnp.min(jnp.where(sc == m, lane_f, float(LANES)), axis=-1, keepdims=True)
            hit = lane_f == pos
            if payload is None:
                out = pos
            else:
                out = jnp.sum(jnp.where(hit, payload, 0.0), axis=-1, keepdims=True)
            vals = jnp.where(lane == it, m, vals)
            idxs = jnp.where(lane == it, out, idxs)
            sc = jnp.where(hit, NEG_INF, sc)
        return vals, idxs

    nt = (((1,), (1,)), ((), ()))
    s1 = lax.dot_general(q_ref[:, :PEER_HALF].astype(BF16), sk_ref[0, 0], nt,
                         preferred_element_type=F32)
    s2 = lax.dot_general(q_ref[:, PEER_HALF:].astype(BF16), sk_ref[0, 1], nt,
                         preferred_element_type=F32)
    v1, i1 = top16(s1, None)
    v2, i2 = top16(s2, None)
    hp = lax.Precision.HIGHEST
    cand = (jnp.dot(v1, e1_ref[...], precision=hp, preferred_element_type=F32)
            + jnp.dot(v2, e2_ref[...], precision=hp, preferred_element_type=F32)
            + cmask_ref[...])
    expert = (jnp.dot(i1, e1_ref[...], precision=hp, preferred_element_type=F32) * float(N_KEYS)
              + jnp.dot(i2, e2_ref[...], precision=hp, preferred_element_type=F32))
    vs, es = top16(cand, expert)
    valid = lane < k
    p = jnp.where(valid, jnp.exp(vs - vs[:, 0:1]), 0.0)
    gate = p / jnp.sum(p, axis=-1, keepdims=True)

    @pl.when(h == 0)
    def _():
        ids_ref[...] = jnp.zeros_like(ids_ref)
        gates_ref[...] = jnp.zeros_like(gates_ref)

    shift = h * k
    own = (lane >= shift) & (lane < shift + k)
    gates_ref[...] = jnp.where(own, pltpu.roll(gate, shift, 1), gates_ref[...])
    ids_ref[...] = jnp.where(own, pltpu.roll(es, shift, 1).astype(jnp.int32), ids_ref[...])


def _topk(q, sub_keys_bf16):
    s = q.shape[0]
    t = min(SEQ_BLOCK, s)
    k = PEER_TOPK
    pairs = _staircase()
    lanes = jnp.arange(LANES)
    pi = jnp.array([p[0] for p in pairs] + [0] * (LANES - len(pairs)))
    pj = jnp.array([p[1] for p in pairs] + [0] * (LANES - len(pairs)))
    live = lanes < len(pairs)
    rows = jnp.arange(LANES)[:, None]
    e1 = ((rows == pi[None, :]) & live[None, :]).astype(F32)
    e2 = ((rows == pj[None, :]) & live[None, :]).astype(F32)
    cmask = jnp.where(live, 0.0, NEG_INF).astype(F32)[None, :]
    const = lambda shape: pl.BlockSpec(shape, lambda i, h: (0,) * len(shape))
    out = pl.BlockSpec((t, LANES), lambda i, h: (i, 0))
    return pl.pallas_call(
        _topk_kernel,
        grid=(s // t, PEER_HEADS),
        in_specs=[pl.BlockSpec((t, 2 * PEER_HALF), lambda i, h: (i, h)),
                  pl.BlockSpec((1, 2, N_KEYS, PEER_HALF), lambda i, h: (h, 0, 0, 0)),
                  const((LANES, LANES)), const((LANES, LANES)), const((1, LANES))],
        out_specs=[out, out],
        out_shape=[jax.ShapeDtypeStruct((s, PEER_HEADS * k), jnp.int32),
                   jax.ShapeDtypeStruct((s, PEER_HEADS * k), F32)],
        compiler_params=_cparams(2),
        name="topk",
    )(q, sub_keys_bf16, e1, e2, cmask)


def _pack_expert_table(peer_u, peer_v):
    lo = lax.bitcast_convert_type(peer_u.astype(BF16), jnp.uint16).astype(jnp.uint32)
    hi = lax.bitcast_convert_type(peer_v.astype(BF16), jnp.uint16).astype(jnp.uint32)
    return lo | (hi << 16)


def _peer_kernel(alpha, ids_ref, nxt_ids_ref, gates_ref, h_ref, tab_hbm, g_ref, b_ref, o_ref,
                 buf_a, buf_b, acc_ref, sem):
    step = pl.program_id(0)
    last_step = pl.num_programs(0) - 1
    tb = h_ref.shape[0]
    group, n_rows = buf_a.shape[0], buf_a.shape[1]
    bufs = (buf_a, buf_b)
    n_pairs = tb // (2 * group)

    def issue_group(ids, first_tok, slot_set):
        for j in range(group):
            for r in range(n_rows):
                e = ids[first_tok + j, r]
                pltpu.make_async_copy(tab_hbm.at[pl.ds(e, 1), :],
                                      bufs[slot_set].at[j, pl.ds(r, 1), :],
                                      sem.at[slot_set, j]).start(priority=r % 2)

    def wait_group(slot_set):
        for j in range(group):
            pltpu.make_async_copy(tab_hbm.at[pl.ds(0, n_rows), :], bufs[slot_set].at[j],
                                  sem.at[slot_set, j]).wait()

    def compute(tok, slot_set, j):
        words = bufs[slot_set][j]
        ub = lax.bitcast_convert_type(words << 16, F32).astype(BF16)
        vb = lax.bitcast_convert_type(words & jnp.uint32(0xFFFF0000), F32).astype(BF16)
        xb = h_ref[pl.ds(tok, 1), :].astype(BF16)
        act = lax.dot_general(xb, ub, (((1,), (1,)), ((), ())), preferred_element_type=F32)
        w = gates_ref[pl.ds(tok, 1), :] * jax.nn.gelu(act)
        acc_ref[pl.ds(tok, 1), :] = jnp.dot(w.astype(BF16), vb, preferred_element_type=F32)

    @pl.when(step == 0)
    def _():
        issue_group(ids_ref, 0, 0)

    def half(first_tok, slot_set, request_next):
        wait_group(slot_set)
        request_next()
        for j in range(group):
            compute(first_tok + j, slot_set, j)

    def pair(pi, carry):
        base = pi * 2 * group
        half(base, 0, lambda: issue_group(ids_ref, base + group, 1))
        half(base + group, 1, lambda: issue_group(ids_ref, base + 2 * group, 0))
        return carry

    lax.fori_loop(0, n_pairs - 1, pair, 0)

    base = tb - 2 * group
    half(base, 0, lambda: issue_group(ids_ref, base + group, 1))

    def request_first_of_next_step():
        @pl.when(step < last_step)
        def _():
            issue_group(nxt_ids_ref, 0, 0)

    half(base + group, 1, request_first_of_next_step)

    o_ref[...] = _layer_norm(alpha * h_ref[...] + acc_ref[...], g_ref[...], b_ref[...])


def _peer(alpha, ids, gates, h1, table, ln_g, ln_b):
    s, d = h1.shape
    tb = min(PEER_BLOCK, s)
    n_rows = ids.shape[1]
    n_steps = s // tb
    assert tb % (2 * GATHER_GROUP) == 0 and tb >= 4 * GATHER_GROUP
    row = lambda w: pl.BlockSpec((tb, w), lambda i: (i, 0))
    const = lambda shape: pl.BlockSpec(shape, lambda i: (0,) * len(shape))
    return pl.pallas_call(
        functools.partial(_peer_kernel, alpha),
        grid=(n_steps,),
        in_specs=[pl.BlockSpec((tb, n_rows), lambda i: (i, 0), memory_space=pltpu.SMEM),
                  pl.BlockSpec((tb, n_rows), lambda i: (jnp.minimum(i + 1, n_steps - 1), 0),
                               memory_space=pltpu.SMEM),
                  row(n_rows), row(d), pl.BlockSpec(memory_space=pl.ANY),
                  const((1, d)), const((1, d))],
        out_specs=row(d),
        out_shape=jax.ShapeDtypeStruct((s, d), F32),
        scratch_shapes=[pltpu.VMEM((GATHER_GROUP, n_rows, d), jnp.uint32),
                        pltpu.VMEM((GATHER_GROUP, n_rows, d), jnp.uint32),
                        pltpu.VMEM((tb, d), F32),
                        pltpu.SemaphoreType.DMA((2, GATHER_GROUP))],
        compiler_params=_cparams(1),
        name="peer",
    )(ids, ids, gates, h1, table, ln_g[None, :], ln_b[None, :])


def kernel(x, positions, w_in, w_out, conv_w, conv_b, w_ra, b_ra, w_ri, b_ri, lru_lambda,
           ln1_g, ln1_b, w_pq, sub_keys, peer_u, peer_v, ln2_g, ln2_b):
    batch, seq, d = x.shape
    assert batch == 1, "one sequence per call"
    depth = w_in.shape[0]
    alpha = (2.0 * depth) ** 0.25
    pos_col = positions.reshape(seq, 1)
    h = x.reshape(seq, d)
    for l in range(depth):
        proj = _proj(h, w_in[l].astype(BF16))
        y_ret = _retention(proj, pos_col)
        wcat = jnp.concatenate([w_ra[l], w_ri[l]], axis=-1).astype(BF16)
        y_lru = _rglru(proj, conv_w[l], conv_b[l], wcat, b_ra[l], b_ri[l], lru_lambda[l])
        h1, q = _mix(alpha, h, y_ret, y_lru, w_out[l].astype(BF16), ln1_g[l], ln1_b[l],
                     w_pq[l].astype(BF16))
        ids, gates = _topk(q, sub_keys[l].astype(BF16))
        h = _peer(alpha, ids, gates, h1, _pack_expert_table(peer_u[l], peer_v[l]), ln2_g[l],
                  ln2_b[l])
    return h.reshape(batch, seq, d)
```
